```python
import math
import jax, jax.numpy as jnp
from jax import lax
import numpy as np

D_MODEL = 4096
BATCH = 1
SEQ = 16384
DEPTH = 2
DEC_BATCH = 4
DEC_SEQ = 2048
PAST_LEN = 128

HEAD_DIM = 128
N_HEADS = D_MODEL // HEAD_DIM
H_NA = N_HEADS // 4
H_MLSTM = (N_HEADS - H_NA) // 2
H_RET = N_HEADS - H_NA - H_MLSTM
W_MLSTM = H_MLSTM * HEAD_DIM
W_NA = H_NA * HEAD_DIM
W_RET = H_RET * HEAD_DIM
D_MIX = W_MLSTM + W_NA + W_RET
N_IN = 4 * W_MLSTM + 4 * H_MLSTM + 3 * W_NA + 4 * W_RET
FFN_HIDDEN = (8 * D_MODEL + 3 * 256 - 1) // (3 * 256) * 256
CHUNK = 128
GRID_W = 64
NA_KH = 8
NA_KW = 16
ROPE_BASE = 10000.0
EPS = 1e-6
N_MOD = 6

kernel_name = 'hybrid_mlstm_natten_retnet_encoder'

F32 = jnp.float32


def rms_norm(x, g):
    xf = x.astype(F32)
    return xf * lax.rsqrt(jnp.mean(xf * xf, axis=-1, keepdims=True) + EPS) * g.astype(F32)


def head_layer_norm(y):
    mu = jnp.mean(y, axis=-1, keepdims=True)
    yc = y - mu
    return yc * lax.rsqrt(jnp.mean(yc * yc, axis=-1, keepdims=True) + EPS)


def rotary(t):
    S, d = t.shape[2], t.shape[3]
    half = d // 2
    inv_freq = jnp.exp(-math.log(ROPE_BASE) * jnp.arange(half, dtype=F32) / half)
    ang = jnp.arange(S, dtype=F32)[:, None] * inv_freq[None, :]
    cos, sin = jnp.cos(ang), jnp.sin(ang)
    t1, t2 = t[..., :half], t[..., half:]
    return jnp.concatenate([t1 * cos - t2 * sin, t1 * sin + t2 * cos], axis=-1)


def flip_t(t):
    return jnp.flip(t, axis=2)


def to_chunks(t):
    B, H, S = t.shape[:3]
    t = t.reshape((B, H, S // CHUNK, CHUNK) + t.shape[3:])
    return jnp.moveaxis(t, 2, 0)


def from_chunks(t):
    t = jnp.moveaxis(t, 0, 2)
    return t.reshape(t.shape[:2] + (-1,) + t.shape[4:])


def mlstm_chunkwise(q, k, v, log_i, log_f):
    B, H, S, d = q.shape
    causal = jnp.tril(jnp.ones((CHUNK, CHUNK), dtype=bool))

    def step(carry, xs):
        C, n, m = carry
        qc, kc, vc, lic, lfc = xs
        b = jnp.cumsum(lfc, axis=-1)
        d_log = b[..., :, None] - b[..., None, :] + lic[..., None, :]
        d_log = jnp.where(causal, d_log, -jnp.inf)
        inter_log = b + m[..., None]
        m_row = jnp.maximum(inter_log, jnp.max(d_log, axis=-1))
        s = jnp.einsum('bhld,bhsd->bhls', qc, kc) * jnp.exp(d_log - m_row[..., None])
        inter_w = jnp.exp(inter_log - m_row)
        num = jnp.einsum('bhls,bhse->bhle', s, vc) + inter_w[..., None] * jnp.einsum('bhld,bhde->bhle', qc, C)
        den = jnp.sum(s, axis=-1) + inter_w * jnp.einsum('bhld,bhd->bhl', qc, n)
        h = num / jnp.maximum(jnp.abs(den), jnp.exp(-m_row))[..., None]
        b_tot = b[..., -1]
        a = b_tot[..., None] - b + lic
        m_new = jnp.maximum(b_tot + m, jnp.max(a, axis=-1))
        decay = jnp.exp(b_tot + m - m_new)
        w = jnp.exp(a - m_new[..., None])
        C_new = decay[..., None, None] * C + jnp.einsum('bhld,bhle->bhde', kc * w[..., None], vc)
        n_new = decay[..., None] * n + jnp.einsum('bhl,bhld->bhd', w, kc)
        return (C_new, n_new, m_new), h

    init = (jnp.zeros((B, H, d, d), F32), jnp.zeros((B, H, d), F32), jnp.zeros((B, H), F32))
    _, h = lax.scan(step, init, (to_chunks(q), to_chunks(k), to_chunks(v), to_chunks(log_i), to_chunks(log_f)))
    return from_chunks(h)


def mlstm_mixer(q, k, v, o, gates, gate_bias):
    B, S = q.shape[:2]
    q, k, v = [jnp.swapaxes(t.astype(F32), 1, 2) for t in (q, k, v)]
    k = k * HEAD_DIM ** -0.5
    g = gates.astype(F32).reshape(B, S, 4, H_MLSTM) + gate_bias.astype(F32)
    g = jnp.transpose(g, (2, 0, 3, 1))
    fwd = mlstm_chunkwise(q, k, v, g[0], jax.nn.log_sigmoid(g[1]))
    bwd = flip_t(mlstm_chunkwise(flip_t(q), flip_t(k), flip_t(v), flip_t(g[2]), flip_t(jax.nn.log_sigmoid(g[3]))))
    y = jnp.swapaxes(head_layer_norm(fwd + bwd), 1, 2).reshape(B, S, W_MLSTM)
    return jax.nn.sigmoid(o.astype(F32)) * y


def retention_chunkwise(q, k, v, log_gamma):
    B, H, S, d = q.shape
    idx = jnp.arange(CHUNK, dtype=F32)
    diff = idx[:, None] - idx[None, :]
    d_intra = jnp.where(diff >= 0, jnp.exp(jnp.maximum(diff, 0.0) * log_gamma[:, None, None]), 0.0)
    w_inter = jnp.exp((idx + 1.0)[None, :] * log_gamma[:, None])[None, :, :, None]
    w_state = jnp.exp((CHUNK - 1.0 - idx)[None, :] * log_gamma[:, None])[None, :, :, None]
    chunk_decay = jnp.exp(CHUNK * log_gamma)[None, :, None, None]

    def step(R, xs):
        qc, kc, vc = xs
        s = jnp.einsum('bhld,bhsd->bhls', qc, kc) * d_intra
        out = jnp.einsum('bhls,bhse->bhle', s, vc) + w_inter * jnp.einsum('bhld,bhde->bhle', qc, R)
        R = chunk_decay * R + jnp.einsum('bhld,bhle->bhde', kc * w_state, vc)
        return R, out

    _, out = lax.scan(step, jnp.zeros((B, H, d, d), F32), (to_chunks(q), to_chunks(k), to_chunks(v)))
    return from_chunks(out)


def retention_mixer(q, k, v, g, decay_logit):
    B, S = q.shape[:2]
    q, k, v = [jnp.swapaxes(t.astype(F32), 1, 2) for t in (q, k, v)]
    q = rotary(q)
    k = rotary(k) * HEAD_DIM ** -0.5
    log_gamma = jax.nn.log_sigmoid(decay_logit.astype(F32))
    fwd = retention_chunkwise(q, k, v, log_gamma[0])
    bwd = flip_t(retention_chunkwise(flip_t(q), flip_t(k), flip_t(v), log_gamma[1]))
    y = jnp.swapaxes(head_layer_norm(fwd + bwd), 1, 2).reshape(B, S, W_RET)
    return jax.nn.silu(g.astype(F32)) * y


def neighborhood_attention(q, k, v, rpb):
    B, S, H, d = q.shape
    rows = S // GRID_W
    kh = min(NA_KH, rows)
    shp = (B, rows, GRID_W, H, d)
    qg, kg, vg = q.reshape(shp), k.reshape(shp), v.reshape(shp)
    cols = np.arange(GRID_W)
    col_start = np.clip(cols - NA_KW // 2, 0, GRID_W - NA_KW)
    col_idx = col_start[:, None] + np.arange(NA_KW)[None, :]
    col_bias_idx = col_idx - cols[:, None] + NA_KW - 1
    scale = d ** -0.5
    rpb = rpb.astype(F32)

    def row_fn(args):
        r, q_row = args
        rs = jnp.clip(r - kh // 2, 0, rows - kh)
        k_win = lax.dynamic_slice_in_dim(kg, rs, kh, axis=1)[:, :, col_idx]
        v_win = lax.dynamic_slice_in_dim(vg, rs, kh, axis=1)[:, :, col_idx]
        s = jnp.einsum('bchd,bicjhd->bhcij', q_row, k_win).astype(F32) * scale
        row_bias_idx = rs + jnp.arange(kh) - r + NA_KH - 1
        bias = rpb[:, row_bias_idx[None, :, None], col_bias_idx[:, None, :]]
        p = jax.nn.softmax(s + bias[None], axis=(-2, -1))
        return jnp.einsum('bhcij,bicjhd->bchd', p.astype(v_win.dtype), v_win)

    out = lax.map(row_fn, (jnp.arange(rows), jnp.moveaxis(qg, 1, 0)))
    return jnp.moveaxis(out, 0, 1).reshape(B, S, H * d).astype(F32)


def mixing_block(h, w_in, gate_bias, decay_logit, rpb, w_out):
    B, S, _ = h.shape
    sizes = [W_MLSTM] * 4 + [4 * H_MLSTM] + [W_NA] * 3 + [W_RET] * 4
    points = [int(p) for p in np.cumsum(sizes)[:-1]]
    (qm, km, vm, om, gm, qa, ka, va, qr, kr, vr, gr) = jnp.split(h @ w_in, points, axis=-1)

    def heads(t):
        return t.reshape(B, S, -1, HEAD_DIM)

    y_m = mlstm_mixer(heads(qm), heads(km), heads(vm), om, gm, gate_bias)
    y_a = neighborhood_attention(heads(qa), heads(ka), heads(va), rpb)
    y_r = retention_mixer(heads(qr), heads(kr), heads(vr), gr, decay_logit)
    y = jnp.concatenate([y_m, y_a, y_r], axis=-1).astype(h.dtype)
    return y @ w_out


def encoder_layer(x, c, w_mod, b_mod, g_mix, g_ffn, w_in, gate_bias, decay_logit, rpb, w_out, w_gate_up, w_down):
    B = x.shape[0]
    mod = (jax.nn.silu(c) @ w_mod + b_mod).astype(F32).reshape(B, N_MOD, 1, D_MODEL)
    shift1, scale1, gate1, shift2, scale2, gate2 = [mod[:, i] for i in range(N_MOD)]
    h = (rms_norm(x, g_mix) * (1.0 + scale1) + shift1).astype(x.dtype)
    mix = mixing_block(h, w_in, gate_bias, decay_logit, rpb, w_out)
    x = (x.astype(F32) + gate1 * mix.astype(F32)).astype(x.dtype)
    h = (rms_norm(x, g_ffn) * (1.0 + scale2) + shift2).astype(x.dtype)
    a, b = jnp.split(h @ w_gate_up, 2, axis=-1)
    ffn = (jax.nn.silu(a) * b) @ w_down
    return (x.astype(F32) + gate2 * ffn.astype(F32)).astype(x.dtype)


def setup_inputs(seed: int = 0) -> dict:
    key = jax.random.key(seed)
    ks = jax.random.split(key, 16)

    def nrm(k, shape, s):
        return jax.random.normal(k, shape, jnp.float32) * s

    f_bias = np.linspace(3.0, 6.0, H_MLSTM).astype(np.float32)
    zeros = np.zeros((H_MLSTM,), np.float32)
    gate_base = np.stack([zeros, f_bias, zeros, f_bias])
    gamma = 1.0 - 2.0 ** (-5.0 - np.arange(H_RET))
    decay_base = np.log(gamma / (1.0 - gamma)).astype(np.float32)
    return {
        'x_prompt': nrm(ks[0], (BATCH, SEQ, D_MODEL), 1.0),
        'x_sample': nrm(ks[1], (DEC_BATCH, DEC_SEQ, D_MODEL), 1.0),
        'c_prompt': nrm(ks[2], (BATCH, D_MODEL), 1.0),
        'c_sample': nrm(ks[3], (DEC_BATCH, D_MODEL), 1.0),
        'w_mod': nrm(ks[4], (DEPTH, D_MODEL, N_MOD * D_MODEL), 0.3 * D_MODEL ** -0.5),
        'b_mod': nrm(ks[5], (DEPTH, N_MOD * D_MODEL), 0.01),
        'g_mix': 1.0 + nrm(ks[6], (DEPTH, D_MODEL), 0.02),
        'g_ffn': 1.0 + nrm(ks[7], (DEPTH, D_MODEL), 0.02),
        'w_in': nrm(ks[8], (DEPTH, D_MODEL, N_IN), D_MODEL ** -0.5),
        'mlstm_gate_bias': jnp.asarray(gate_base)[None] + nrm(ks[9], (DEPTH, 4, H_MLSTM), 0.1),
        'ret_decay_logit': jnp.asarray(decay_base)[None, None] + nrm(ks[10], (DEPTH, 2, H_RET), 0.1),
        'na_rpb': nrm(ks[11], (DEPTH, H_NA, 2 * NA_KH - 1, 2 * NA_KW - 1), 0.02),
        'w_out': nrm(ks[12], (DEPTH, D_MIX, D_MODEL), D_MIX ** -0.5),
        'w_gate_up': nrm(ks[13], (DEPTH, D_MODEL, 2 * FFN_HIDDEN), D_MODEL ** -0.5),
        'w_down': nrm(ks[14], (DEPTH, FFN_HIDDEN, D_MODEL), FFN_HIDDEN ** -0.5),
        'g_final': 1.0 + nrm(ks[15], (D_MODEL,), 0.02),
    }


def reference(x_prompt, x_sample, c_prompt, c_sample, w_mod, b_mod, g_mix, g_ffn, w_in, mlstm_gate_bias,
              ret_decay_logit, na_rpb, w_out, w_gate_up, w_down, g_final):
    def trunk(x, c):
        for l in range(DEPTH):
            x = encoder_layer(x, c, w_mod[l], b_mod[l], g_mix[l], g_ffn[l], w_in[l], mlstm_gate_bias[l],
                              ret_decay_logit[l], na_rpb[l], w_out[l], w_gate_up[l], w_down[l])
        return rms_norm(x, g_final).astype(x.dtype)

    y_prompt = trunk(x_prompt, c_prompt)
    y_sample = trunk(x_sample, c_sample)
    return (y_prompt, y_sample)
```

```python
import functools
import math

import numpy as np
import jax
import jax.numpy as jnp
from jax import lax
from jax.experimental import pallas as pl
from jax.experimental.pallas import tpu as pltpu

F32 = jnp.float32
BF16 = jnp.bfloat16

LANES_V7X = 128
VMEM_LIMIT_V7X = 56 * 1024 * 1024

HEAD_DIM = 128
N_MOD = 6
CHUNK = 128
GRID_W = 64
NA_KH = 8
NA_KW = 16
NA_QROWS = 8
NA_KROWS = 16
ROPE_BASE = 10000.0
EPS = 1e-6
NEG = -1e30


def _params(semantics):
    return pltpu.CompilerParams(dimension_semantics=semantics, vmem_limit_bytes=VMEM_LIMIT_V7X)


def _seq_of_row(row, p_rows, d_rows):
    return jnp.where(row < p_rows, 0, 1 + (row - p_rows) // d_rows)


def _tile(n, preferred):
    best = 0
    for cand in range(LANES_V7X, min(n, preferred) + 1, LANES_V7X):
        if n % cand == 0:
            best = cand
    assert best > 0, (n, preferred)
    return best


def _silu(x):
    return x * jax.nn.sigmoid(x)


def _log_sigmoid(x):
    return jnp.minimum(x, 0.0) - jnp.log1p(jnp.exp(-jnp.abs(x)))


def _mod_body(c_ref, w_ref, b_ref, o_ref):
    c = c_ref[...]
    o_ref[0] = jnp.dot(_silu(c).astype(BF16), w_ref[0].astype(BF16), preferred_element_type=F32) + b_ref[0]


def _modulation(c_all, w_mod, b_mod, *, tn=512):
    depth, d, n = w_mod.shape
    return pl.pallas_call(
        _mod_body,
        grid=(depth, n // tn),
        in_specs=[pl.BlockSpec((8, d), lambda l, j: (0, 0)),
                  pl.BlockSpec((1, d, tn), lambda l, j: (l, 0, j)),
                  pl.BlockSpec((1, 1, tn), lambda l, j: (l, 0, j))],
        out_specs=pl.BlockSpec((1, 8, tn), lambda l, j: (l, 0, j)),
        out_shape=jax.ShapeDtypeStruct((depth, 8, n), F32),
        compiler_params=_params(("arbitrary", "arbitrary")),
        name="modulation",
    )(c_all, w_mod, b_mod)


def _norm_body(*refs, modulate, i_shift, i_scale):
    if modulate:
        x_ref, g_ref, mod_ref, o_ref = refs
    else:
        x_ref, g_ref, o_ref = refs
    x = x_ref[...]
    y = x * lax.rsqrt(jnp.mean(x * x, axis=-1, keepdims=True) + EPS) * g_ref[...]
    if modulate:
        y = y * (1.0 + mod_ref[0, i_scale:i_scale + 1, :]) + mod_ref[0, i_shift:i_shift + 1, :]
    o_ref[...] = y.astype(o_ref.dtype)


def _rms_norm(x, g, mod, *, i_shift, i_scale, p_rows, d_rows, out_dtype, tm=256):
    t, d = x.shape
    modulate = mod is not None
    in_specs = [pl.BlockSpec((tm, d), lambda i: (i, 0)), pl.BlockSpec((1, d), lambda i: (0, 0))]
    args = [x, g]
    if modulate:
        in_specs.append(pl.BlockSpec((1, N_MOD, d), lambda i: (_seq_of_row(i * tm, p_rows, d_rows), 0, 0)))
        args.append(mod)
    return pl.pallas_call(
        functools.partial(_norm_body, modulate=modulate, i_shift=i_shift, i_scale=i_scale),
        grid=(t // tm,),
        in_specs=in_specs,
        out_specs=pl.BlockSpec((tm, d), lambda i: (i, 0)),
        out_shape=jax.ShapeDtypeStruct((t, d), out_dtype),
        compiler_params=_params(("arbitrary",)),
        name="rms_norm",
    )(*args)


def _mm_body(*refs, n_w, nk, epilogue, i_gate):
    x_ref = refs[0]
    w_refs = refs[1:1 + n_w]
    pos = 1 + n_w
    if epilogue == "resid":
        res_ref, mod_ref = refs[pos], refs[pos + 1]
        pos += 2
    o_ref = refs[pos]
    acc_refs = refs[pos + 1:]

    x = x_ref[...]
    parts = [jnp.dot(x, w[...], preferred_element_type=F32) for w in w_refs]

    def finish(accs):
        if epilogue == "plain":
            o_ref[...] = accs[0].astype(o_ref.dtype)
        elif epilogue == "swiglu":
            o_ref[...] = (_silu(accs[0]) * accs[1]).astype(o_ref.dtype)
        else:
            o_ref[...] = res_ref[...] + mod_ref[0, i_gate:i_gate + 1, :] * accs[0]

    if nk == 1:
        finish(parts)
    else:
        k = pl.program_id(2)

        @pl.when(k == 0)
        def _():
            for a, p in zip(acc_refs, parts):
                a[...] = p

        @pl.when(k > 0)
        def _():
            for a, p in zip(acc_refs, parts):
                a[...] += p

        @pl.when(k == nk - 1)
        def _():
            finish([a[...] for a in acc_refs])


def _matmul(x, ws, *, tm, tn, tk, out_dtype, epilogue="plain", res=None, mod=None, i_gate=0,
            p_rows=0, d_rows=1, name="matmul"):
    m, kdim = x.shape
    n = ws[0].shape[1]
    tn = _tile(n, tn)
    nk = kdim // tk
    assert m % tm == 0 and n % tn == 0 and kdim % tk == 0
    in_specs = [pl.BlockSpec((tm, tk), lambda i, j, k: (i, k))]
    in_specs += [pl.BlockSpec((tk, tn), lambda i, j, k: (k, j)) for _ in ws]
    args = [x, *ws]
    if epilogue == "resid":
        in_specs.append(pl.BlockSpec((tm, tn), lambda i, j, k: (i, j)))
        in_specs.append(pl.BlockSpec((1, N_MOD, tn), lambda i, j, k: (_seq_of_row(i * tm, p_rows, d_rows), 0, j)))
        args += [res, mod]
    scratch = [pltpu.VMEM((tm, tn), F32) for _ in ws] if nk > 1 else []
    return pl.pallas_call(
        functools.partial(_mm_body, n_w=len(ws), nk=nk, epilogue=epilogue, i_gate=i_gate),
        grid=(m // tm, n // tn, nk),
        in_specs=in_specs,
        out_specs=pl.BlockSpec((tm, tn), lambda i, j, k: (i, j)),
        out_shape=jax.ShapeDtypeStruct((m, n), out_dtype),
        scratch_shapes=scratch,
        compiler_params=_params(("arbitrary", "arbitrary", "arbitrary")),
        name=name,
    )(*args)


def _gate_body(h_ref, wi_ref, wf_ref, bi_ref, bf_ref, ucol_ref, bcol_ref, urow_ref, *, n_heads, chunk, rows_out):
    h = h_ref[...]
    gi = jnp.dot(h, wi_ref[...], preferred_element_type=F32) + bi_ref[...]
    lf = _log_sigmoid(jnp.dot(h, wf_ref[...], preferred_element_type=F32) + bf_ref[...])
    is_bwd = lax.broadcasted_iota(jnp.int32, (1, LANES_V7X), 1) >= n_heads
    r = lax.broadcasted_iota(jnp.int32, (chunk, chunk), 0)
    c = lax.broadcasted_iota(jnp.int32, (chunk, chunk), 1)
    tri = jnp.where(c <= r, 1.0, 0.0).astype(BF16)
    for ci in range(h.shape[0] // chunk):
        sl = slice(ci * chunk, (ci + 1) * chunk)
        lfc = lf[sl]
        hi = lfc.astype(BF16)
        r1 = lfc - hi.astype(F32)
        mid = r1.astype(BF16)
        lo = (r1 - mid.astype(F32)).astype(BF16)
        cs = (jnp.dot(tri, hi, preferred_element_type=F32) + jnp.dot(tri, mid, preferred_element_type=F32)
              + jnp.dot(tri, lo, preferred_element_type=F32))
        b = jnp.where(is_bwd, cs[chunk - 1:chunk, :] - cs + lfc, cs)
        u = gi[sl] - b
        ucol_ref[sl, :] = u
        bcol_ref[sl, :] = b
        urow_ref[:, sl] = u.T[:rows_out, :]


def _mlstm_gates(h, w_i, w_f, b_i, b_f, *, n_heads, tm=1024):
    t, d = h.shape
    rows_out = 32
    assert 2 * n_heads <= rows_out
    return pl.pallas_call(
        functools.partial(_gate_body, n_heads=n_heads, chunk=CHUNK, rows_out=rows_out),
        grid=(t // tm,),
        in_specs=[pl.BlockSpec((tm, d), lambda i: (i, 0)),
                  pl.BlockSpec((d, LANES_V7X), lambda i: (0, 0)),
                  pl.BlockSpec((d, LANES_V7X), lambda i: (0, 0)),
                  pl.BlockSpec((1, LANES_V7X), lambda i: (0, 0)),
                  pl.BlockSpec((1, LANES_V7X), lambda i: (0, 0))],
        out_specs=[pl.BlockSpec((tm, LANES_V7X), lambda i: (i, 0)),
                   pl.BlockSpec((tm, LANES_V7X), lambda i: (i, 0)),
                   pl.BlockSpec((rows_out, tm), lambda i: (0, i))],
        out_shape=[jax.ShapeDtypeStruct((t, LANES_V7X), F32),
                   jax.ShapeDtypeStruct((t, LANES_V7X), F32),
                   jax.ShapeDtypeStruct((rows_out, t), F32)],
        compiler_params=_params(("arbitrary",)),
        name="mlstm_gates",
    )(h, w_i, w_f, b_i, b_f)


def _chunk_maps(reverse, n_chunks, p_chunks, d_chunks):
    def chunk_of(t):
        return n_chunks - 1 - t if reverse else t

    def starts_sequence(c):
        edge = d_chunks - 1 if reverse else 0
        p_edge = p_chunks - 1 if reverse else 0
        return (c == p_edge) | ((c >= p_chunks) & ((c - p_chunks) % d_chunks == edge))

    return chunk_of, starts_sequence


def _head_norm(y):
    yc = y - jnp.mean(y, axis=-1, keepdims=True)
    return yc * lax.rsqrt(jnp.mean(yc * yc, axis=-1, keepdims=True) + EPS)


def _mlstm_body(*refs, reverse, n_heads, chunk, starts_sequence, chunk_of):
    if reverse:
        q_ref, k_ref, v_ref, ucol_ref, bcol_ref, urow_ref, hf_ref, o_ref, y_ref, c_ref, m_ref = refs
    else:
        q_ref, k_ref, v_ref, ucol_ref, bcol_ref, urow_ref, y_ref, c_ref, m_ref = refs
    d = HEAD_DIM
    ch = chunk_of(pl.program_id(0))

    @pl.when(starts_sequence(ch))
    def _():
        c_ref[...] = jnp.zeros_like(c_ref)
        m_ref[...] = jnp.zeros_like(m_ref)

    row = lax.broadcasted_iota(jnp.int32, (chunk, chunk), 0)
    col = lax.broadcasted_iota(jnp.int32, (chunk, chunk), 1)
    visible = (col >= row) if reverse else (col <= row)
    edge = 0 if reverse else chunk - 1
    ones = jnp.ones((chunk, d), BF16)
    scale = HEAD_DIM ** -0.5

    for h in range(n_heads):
        hs = slice(h * d, (h + 1) * d)
        g = h + n_heads if reverse else h
        qb = q_ref[:, hs].astype(BF16)
        kc = k_ref[:, hs] * scale
        v_aug = jnp.concatenate([v_ref[:, hs].astype(BF16), ones], axis=1)
        a = lax.dot_general(qb, kc.astype(BF16), (((1,), (1,)), ((), ())), preferred_element_type=F32)
        u_vis = jnp.where(visible, urow_ref[g:g + 1, :], NEG)
        m_prev = m_ref[0:1, h:h + 1]
        m_run = jnp.maximum(jnp.max(u_vis, axis=1, keepdims=True), m_prev)
        s = a * jnp.exp(u_vis - m_run)
        intra = jnp.dot(s.astype(BF16), v_aug, preferred_element_type=F32)
        inter = jnp.dot(qb, c_ref[h].astype(BF16), preferred_element_type=F32)
        tot = intra + jnp.exp(m_prev - m_run) * inter
        b_col = bcol_ref[:, g:g + 1]
        floor = jnp.exp(-b_col - m_run)
        h_cur = tot[:, :d] / jnp.maximum(jnp.abs(tot[:, d:]), floor)

        m_edge = m_run[edge:edge + 1, :]
        kw = (kc * jnp.exp(ucol_ref[:, g:g + 1] - m_edge)).astype(BF16)
        upd = lax.dot_general(kw, v_aug, (((0,), (0,)), ((), ())), preferred_element_type=F32)
        c_ref[h] = jnp.exp(m_prev - m_edge) * c_ref[h] + upd
        m_ref[0:1, h:h + 1] = b_col[edge:edge + 1, :] + m_edge

        if reverse:
            y = jax.nn.sigmoid(o_ref[:, hs]) * _head_norm(hf_ref[:, hs] + h_cur)
            y_ref[:, hs] = y.astype(y_ref.dtype)
        else:
            y_ref[:, hs] = h_cur


def _mlstm_direction(xm, u_col, b_col, u_row, h_fwd, *, reverse, n_heads, p_rows, d_rows):
    t = xm.shape[0]
    w = n_heads * HEAD_DIM
    n_chunks = t // CHUNK
    chunk_of, starts_sequence = _chunk_maps(reverse, n_chunks, p_rows // CHUNK, d_rows // CHUNK)

    def col_block(j):
        return pl.BlockSpec((CHUNK, w), lambda i: (chunk_of(i), j))

    in_specs = [col_block(0), col_block(1), col_block(2),
                pl.BlockSpec((CHUNK, LANES_V7X), lambda i: (chunk_of(i), 0)),
                pl.BlockSpec((CHUNK, LANES_V7X), lambda i: (chunk_of(i), 0)),
                pl.BlockSpec((u_row.shape[0], CHUNK), lambda i: (0, chunk_of(i)))]
    args = [xm, xm, xm, u_col, b_col, u_row]
    if reverse:
        in_specs += [col_block(0), col_block(3)]
        args += [h_fwd, xm]
    return pl.pallas_call(
        functools.partial(_mlstm_body, reverse=reverse, n_heads=n_heads, chunk=CHUNK,
                          starts_sequence=starts_sequence, chunk_of=chunk_of),
        grid=(n_chunks,),
        in_specs=in_specs,
        out_specs=col_block(0),
        out_shape=jax.ShapeDtypeStruct((t, w), BF16 if reverse else F32),
        scratch_shapes=[pltpu.VMEM((n_heads, HEAD_DIM, 2 * HEAD_DIM), F32), pltpu.VMEM((8, LANES_V7X), F32)],
        compiler_params=_params(("arbitrary",)),
        name="mlstm_bwd" if reverse else "mlstm_fwd",
    )(*args)


def _ret_body(*refs, reverse, n_heads, chunk, starts_sequence, chunk_of):
    if reverse:
        q_ref, k_ref, v_ref, cos_ref, sin_ref, dl_ref, hf_ref, g_ref, y_ref, r_ref, dmat_ref, wi_ref, ws_ref = refs
    else:
        q_ref, k_ref, v_ref, cos_ref, sin_ref, dl_ref, y_ref, r_ref, dmat_ref, wi_ref, ws_ref = refs
    d = HEAD_DIM
    step = pl.program_id(0)
    ch = chunk_of(step)
    direction = 1 if reverse else 0

    @pl.when(step == 0)
    def _():
        row = lax.broadcasted_iota(jnp.int32, (chunk, chunk), 0)
        col = lax.broadcasted_iota(jnp.int32, (chunk, chunk), 1)
        dist = ((col - row) if reverse else (row - col)).astype(F32)
        pos = lax.broadcasted_iota(jnp.int32, (chunk, d), 0).astype(F32)
        seen = (chunk - 1.0 - pos) if reverse else pos
        for h in range(n_heads):
            lg = _log_sigmoid(dl_ref[direction:direction + 1, h:h + 1])
            dmat_ref[h] = jnp.where(dist >= 0, jnp.exp(jnp.maximum(dist, 0.0) * lg), 0.0)
            wi_ref[h] = jnp.exp((seen + 1.0) * lg)
            ws_ref[h] = jnp.exp((chunk - 1.0 - seen) * lg)

    @pl.when(starts_sequence(ch))
    def _():
        r_ref[...] = jnp.zeros_like(r_ref)

    cos = cos_ref[...]
    sin = sin_ref[...]
    scale = HEAD_DIM ** -0.5

    for h in range(n_heads):
        hs = slice(h * d, (h + 1) * d)
        q = q_ref[:, hs]
        k = k_ref[:, hs]
        qr = q * cos + pltpu.roll(q, d // 2, 1) * sin
        kr = (k * cos + pltpu.roll(k, d // 2, 1) * sin) * scale
        qb = qr.astype(BF16)
        vb = v_ref[:, hs].astype(BF16)
        a = lax.dot_general(qb, kr.astype(BF16), (((1,), (1,)), ((), ())), preferred_element_type=F32)
        s = a * dmat_ref[h]
        out = (jnp.dot(s.astype(BF16), vb, preferred_element_type=F32)
               + wi_ref[h] * jnp.dot(qb, r_ref[h].astype(BF16), preferred_element_type=F32))
        lg = _log_sigmoid(dl_ref[direction:direction + 1, h:h + 1])
        upd = lax.dot_general((kr * ws_ref[h]).astype(BF16), vb, (((0,), (0,)), ((), ())),
                              preferred_element_type=F32)
        r_ref[h] = jnp.exp(chunk * lg) * r_ref[h] + upd
        if reverse:
            y = _silu(g_ref[:, hs]) * _head_norm(hf_ref[:, hs] + out)
            y_ref[:, hs] = y.astype(y_ref.dtype)
        else:
            y_ref[:, hs] = out


def _retention_direction(xr, cos_t, sin_t, decay_logit, h_fwd, *, reverse, n_heads, p_rows, d_rows):
    t = xr.shape[0]
    w = n_heads * HEAD_DIM
    n_chunks = t // CHUNK
    p_chunks, d_chunks = p_rows // CHUNK, d_rows // CHUNK
    chunk_of, starts_sequence = _chunk_maps(reverse, n_chunks, p_chunks, d_chunks)

    def col_block(j):
        return pl.BlockSpec((CHUNK, w), lambda i: (chunk_of(i), j))

    def pos_block():
        def index(i):
            c = chunk_of(i)
            return (jnp.where(c < p_chunks, c, (c - p_chunks) % d_chunks), 0)
        return pl.BlockSpec((CHUNK, HEAD_DIM), index)

    in_specs = [col_block(0), col_block(1), col_block(2), pos_block(), pos_block(),
                pl.BlockSpec(decay_logit.shape, lambda i: (0, 0))]
    args = [xr, xr, xr, cos_t, sin_t, decay_logit]
    if reverse:
        in_specs += [col_block(0), col_block(3)]
        args += [h_fwd, xr]
    return pl.pallas_call(
        functools.partial(_ret_body, reverse=reverse, n_heads=n_heads, chunk=CHUNK,
                          starts_sequence=starts_sequence, chunk_of=chunk_of),
        grid=(n_chunks,),
        in_specs=in_specs,
        out_specs=col_block(0),
        out_shape=jax.ShapeDtypeStruct((t, w), BF16 if reverse else F32),
        scratch_shapes=[pltpu.VMEM((n_heads, HEAD_DIM, HEAD_DIM), F32),
                        pltpu.VMEM((n_heads, CHUNK, CHUNK), F32),
                        pltpu.VMEM((n_heads, CHUNK, HEAD_DIM), F32),
                        pltpu.VMEM((n_heads, CHUNK, HEAD_DIM), F32)],
        compiler_params=_params(("arbitrary",)),
        name="retention_bwd" if reverse else "retention_fwd",
    )(*args)


def _rotary_tables(s_max):
    half = HEAD_DIM // 2
    inv_freq = jnp.exp(-math.log(ROPE_BASE) * jnp.arange(half, dtype=F32) / half)
    ang = jnp.arange(s_max, dtype=F32)[:, None] * inv_freq[None, :]
    cos, sin = jnp.cos(ang), jnp.sin(ang)
    return jnp.concatenate([cos, cos], axis=-1), jnp.concatenate([-sin, sin], axis=-1)


def _na_key_start(i, n_blocks, rows):
    return jnp.where(i == 0, 0, jnp.where(i == n_blocks - 1, rows - NA_KROWS, NA_QROWS * i - NA_KH // 2))


def _na_variant(i, n_blocks):
    return jnp.where(i == 0, 0, jnp.where(i == n_blocks - 1, 2, 1))


def _na_bias_masks(rpb):
    w, qr, kr = GRID_W, NA_QROWS, NA_KROWS
    rows = 4 * qr
    cols = np.arange(w)
    col_start = np.clip(cols - NA_KW // 2, 0, w - NA_KW)
    kc = np.arange(w)
    col_ok = (kc[None, :] >= col_start[:, None]) & (kc[None, :] < col_start[:, None] + NA_KW)
    col_sel = np.zeros((2 * NA_KW - 1, w, w), np.float32)
    for q in range(w):
        for k in range(w):
            if col_ok[q, k]:
                col_sel[k - q + NA_KW - 1, q, k] = 1.0
    row_sel = np.zeros((3, qr, kr, 2 * NA_KH - 1), np.float32)
    row_ok = np.zeros((3, qr, kr), bool)
    for v, (r0, ks) in enumerate([(0, 0), (qr, qr - NA_KH // 2), (rows - qr, rows - kr)]):
        for j in range(qr):
            q_row = r0 + j
            rs = int(np.clip(q_row - NA_KH // 2, 0, rows - NA_KH))
            for i in range(kr):
                k_row = ks + i
                if rs <= k_row < rs + NA_KH:
                    row_ok[v, j, i] = True
                    row_sel[v, j, i, k_row - q_row + NA_KH - 1] = 1.0
    toeplitz = jnp.einsum("hrd,dqk->hrqk", rpb.astype(F32), col_sel, precision=lax.Precision.HIGHEST)
    bias = jnp.einsum("hrqk,vjir->hvjqik", toeplitz, row_sel, precision=lax.Precision.HIGHEST)
    ok = row_ok[:, :, None, :, None] & col_ok[None, None, :, None, :]
    bias = jnp.where(ok[None], bias, NEG)
    return bias.reshape(rpb.shape[0], 3, qr * w, kr * w)


def _na_body(q_ref, k_ref, v_ref, bm_ref, o_ref, *, rows, n_blocks):
    i = pl.program_id(2)
    start = pl.multiple_of(_na_key_start(i, n_blocks, rows) * GRID_W, GRID_W)
    n_keys = NA_KROWS * GRID_W
    kw = k_ref[pl.ds(start, n_keys), :]
    vw = v_ref[pl.ds(start, n_keys), :]
    s = lax.dot_general(q_ref[...], kw, (((1,), (1,)), ((), ())), preferred_element_type=F32)
    s = s * (HEAD_DIM ** -0.5) + bm_ref[0, 0]
    p = jnp.exp(s - jnp.max(s, axis=-1, keepdims=True))
    o = jnp.dot(p.astype(BF16), vw, preferred_element_type=F32) / jnp.sum(p, axis=-1, keepdims=True)
    o_ref[...] = o.astype(o_ref.dtype)


def _neighborhood_attention(xa, bias_masks, *, n_heads, row0, seq_len, n_seq):
    rows = seq_len // GRID_W
    tq = NA_QROWS * GRID_W
    n_blocks = seq_len // tq
    assert rows >= NA_KROWS + NA_QROWS and row0 % seq_len == 0
    q0, s0 = row0 // tq, row0 // seq_len
    return pl.pallas_call(
        functools.partial(_na_body, rows=rows, n_blocks=n_blocks),
        grid=(n_seq, n_heads, n_blocks),
        in_specs=[pl.BlockSpec((tq, HEAD_DIM), lambda b, h, i: (q0 + b * n_blocks + i, h)),
                  pl.BlockSpec((seq_len, HEAD_DIM), lambda b, h, i: (s0 + b, n_heads + h)),
                  pl.BlockSpec((seq_len, HEAD_DIM), lambda b, h, i: (s0 + b, 2 * n_heads + h)),
                  pl.BlockSpec((1, 1, tq, NA_KROWS * GRID_W), lambda b, h, i: (h, _na_variant(i, n_blocks), 0, 0))],
        out_specs=pl.BlockSpec((tq, HEAD_DIM), lambda b, h, i: (b * n_blocks + i, h)),
        out_shape=jax.ShapeDtypeStruct((n_seq * seq_len, n_heads * HEAD_DIM), BF16),
        compiler_params=_params(("arbitrary", "arbitrary", "arbitrary")),
        name="neighborhood_attention",
    )(xa, xa, xa, bias_masks)


def _pad_cols(w, n):
    return jnp.pad(w, ((0, 0), (0, n - w.shape[1])))


def _encoder(x, c_all, w_mod, b_mod, g_mix, g_ffn, w_in, mlstm_gate_bias, ret_decay_logit, na_rpb, w_out,
             w_gate_up, w_down, g_final, *, p_rows, d_rows, n_dec, h_mlstm, h_na, h_ret, tm=1024):
    t, d_model = x.shape
    depth = w_mod.shape[0]
    w_m, w_a, w_r = h_mlstm * HEAD_DIM, h_na * HEAD_DIM, h_ret * HEAD_DIM
    ffn = w_down.shape[1]
    ffn_pad = -(-ffn // 1024) * 1024
    seqs = dict(p_rows=p_rows, d_rows=d_rows)

    mod_all = _modulation(c_all, w_mod, b_mod.reshape(depth, 1, -1)).reshape(depth, 8, N_MOD, d_model)
    cos_t, sin_t = _rotary_tables(max(p_rows, d_rows))

    for l in range(depth):
        mod = mod_all[l]
        o_g = 4 * w_m
        o_a = o_g + 4 * h_mlstm
        o_r = o_a + 3 * w_a
        wl = w_in[l]
        w_in_m = wl[:, :o_g].astype(BF16)
        w_in_a = wl[:, o_a:o_r].astype(BF16)
        w_in_r = wl[:, o_r:].astype(BF16)
        wg = wl[:, o_g:o_a]
        hm = h_mlstm
        w_gi = _pad_cols(jnp.concatenate([wg[:, 0:hm], wg[:, 2 * hm:3 * hm]], axis=1), LANES_V7X).astype(BF16)
        w_gf = _pad_cols(jnp.concatenate([wg[:, hm:2 * hm], wg[:, 3 * hm:4 * hm]], axis=1), LANES_V7X).astype(BF16)
        gb = mlstm_gate_bias[l].astype(F32)
        b_gi = _pad_cols(jnp.concatenate([gb[0], gb[2]])[None], LANES_V7X)
        b_gf = _pad_cols(jnp.concatenate([gb[1], gb[3]])[None], LANES_V7X)
        decay = jnp.pad(ret_decay_logit[l].astype(F32), ((0, 6), (0, LANES_V7X - h_ret)))
        bias_masks = _na_bias_masks(na_rpb[l])
        w_o = w_out[l].astype(BF16)
        w_up_a = _pad_cols(w_gate_up[l][:, :ffn], ffn_pad).astype(BF16)
        w_up_b = _pad_cols(w_gate_up[l][:, ffn:], ffn_pad).astype(BF16)
        w_dn = jnp.pad(w_down[l], ((0, ffn_pad - ffn), (0, 0))).astype(BF16)

        h = _rms_norm(x, g_mix[l][None], mod, i_shift=0, i_scale=1, out_dtype=BF16, **seqs)
        xm = _matmul(h, [w_in_m], tm=tm, tn=512, tk=d_model, out_dtype=F32, name="in_proj_mlstm")
        xa = _matmul(h, [w_in_a], tm=tm, tn=512, tk=d_model, out_dtype=BF16, name="in_proj_na")
        xr = _matmul(h, [w_in_r], tm=tm, tn=512, tk=d_model, out_dtype=F32, name="in_proj_ret")
        u_col, b_col, u_row = _mlstm_gates(h, w_gi, w_gf, b_gi, b_gf, n_heads=h_mlstm)

        hf = _mlstm_direction(xm, u_col, b_col, u_row, None, reverse=False, n_heads=h_mlstm, **seqs)
        y_m = _mlstm_direction(xm, u_col, b_col, u_row, hf, reverse=True, n_heads=h_mlstm, **seqs)
        rf = _retention_direction(xr, cos_t, sin_t, decay, None, reverse=False, n_heads=h_ret, **seqs)
        y_r = _retention_direction(xr, cos_t, sin_t, decay, rf, reverse=True, n_heads=h_ret, **seqs)
        y_a = jnp.concatenate([
            _neighborhood_attention(xa, bias_masks, n_heads=h_na, row0=0, seq_len=p_rows, n_seq=1),
            _neighborhood_attention(xa, bias_masks, n_heads=h_na, row0=p_rows, seq_len=d_rows, n_seq=n_dec)],
            axis=0)
        y = jnp.concatenate([y_m, y_a, y_r], axis=1)
        x = _matmul(y, [w_o], tm=tm, tn=512, tk=y.shape[1], out_dtype=F32, epilogue="resid", res=x, mod=mod,
                    i_gate=2, name="out_proj", **seqs)

        h = _rms_norm(x, g_ffn[l][None], mod, i_shift=3, i_scale=4, out_dtype=BF16, **seqs)
        act = _matmul(h, [w_up_a, w_up_b], tm=tm, tn=512, tk=d_model, out_dtype=BF16, epilogue="swiglu",
                      name="ffn_up")
        x = _matmul(act, [w_dn], tm=tm, tn=512, tk=ffn_pad // 4, out_dtype=F32, epilogue="resid", res=x, mod=mod,
                    i_gate=5, name="ffn_down", **seqs)

    return _rms_norm(x, g_final[None], None, i_shift=0, i_scale=0, out_dtype=F32, **seqs)


def kernel(x_prompt, x_sample, c_prompt, c_sample, w_mod, b_mod, g_mix, g_ffn, w_in, mlstm_gate_bias,
           ret_decay_logit, na_rpb, w_out, w_gate_up, w_down, g_final):
    b_p, s_p, d_model = x_prompt.shape
    b_d, s_d, _ = x_sample.shape
    assert b_p == 1
    n_heads = d_model // HEAD_DIM
    h_na = n_heads // 4
    h_mlstm = (n_heads - h_na) // 2
    h_ret = n_heads - h_na - h_mlstm
    x = jnp.concatenate([x_prompt.reshape(s_p, d_model), x_sample.reshape(b_d * s_d, d_model)], axis=0)
    c_all = jnp.concatenate([c_prompt, c_sample, jnp.zeros((8 - b_p - b_d, d_model), c_prompt.dtype)], axis=0)
    y = _encoder(x.astype(F32), c_all.astype(F32), w_mod, b_mod, g_mix, g_ffn, w_in, mlstm_gate_bias,
                 ret_decay_logit, na_rpb, w_out, w_gate_up, w_down, g_final,
                 p_rows=s_p, d_rows=s_d, n_dec=b_d, h_mlstm=h_mlstm, h_na=h_na, h_ret=h_ret)
    return y[:s_p].reshape(b_p, s_p, d_model), y[s_p:].reshape(b_d, s_d, d_model)
```
